```python
import math
import jax, jax.numpy as jnp
from jax import lax
import numpy as np

D_MODEL = 2048
BATCH = 2
SEQ = 8192
DEPTH = 1
DEC_BATCH = 32
DEC_SEQ = 32
PAST_LEN = 4096

CHUNK = 64
N_HEADS = 8
D_QK = 64
D_V = 2 * D_QK
ATTN_WIDTH = N_HEADS * D_V
CONV_CH = 1024
CONV_WIDTH = 31
CONV_STATE = CONV_WIDTH - 1
ROPE_DIM = D_QK // 4
ROPE_THETA = 500000.0
D_FF = -(-8 * D_MODEL // (3 * 256)) * 256
Q_COLS = N_HEADS * 2 * D_QK
K_COLS = N_HEADS * 2 * D_QK
V_COLS = N_HEADS * D_V
CONV_COLS = 2 * CONV_CH
GATE_COLS = 2 * D_MODEL
IN_COLS = Q_COLS + K_COLS + V_COLS + CONV_COLS + GATE_COLS
Q_BLOCK = 128
EPS = 1e-6

kernel_name = "diff_attn_conformer_conv_hybrid_step"


def lambda_init(layer):
    return 0.8 - 0.6 * math.exp(-0.3 * layer)


def rms_norm(x, g):
    xf = x.astype(jnp.float32)
    y = xf * lax.rsqrt(jnp.mean(xf * xf, axis=-1, keepdims=True) + EPS)
    return (y * g.astype(jnp.float32)).astype(x.dtype)


def layer_norm(x, g, b):
    xf = x.astype(jnp.float32)
    mu = jnp.mean(xf, axis=-1, keepdims=True)
    var = jnp.mean(jnp.square(xf - mu), axis=-1, keepdims=True)
    y = (xf - mu) * lax.rsqrt(var + EPS)
    return (y * g.astype(jnp.float32) + b.astype(jnp.float32)).astype(x.dtype)


def partial_rope(x, pos):
    half = ROPE_DIM // 2
    inv_freq = ROPE_THETA ** (-jnp.arange(0, ROPE_DIM, 2, dtype=jnp.float32) / ROPE_DIM)
    ang = pos.astype(jnp.float32)[:, None] * inv_freq[None, :]
    cos = jnp.cos(ang)[:, None, None, :]
    sin = jnp.sin(ang)[:, None, None, :]
    xf = x.astype(jnp.float32)
    x1 = xf[..., :half]
    x2 = xf[..., half:ROPE_DIM]
    rot = jnp.concatenate([x1 * cos - x2 * sin, x2 * cos + x1 * sin, xf[..., ROPE_DIM:]], axis=-1)
    return rot.astype(x.dtype)


def diff_attention(q, k, v, q_pos, k_pos, lam):
    scores = jnp.einsum('bqhmd,bkhmd->bhmqk', q, k).astype(jnp.float32) * (D_QK ** -0.5)
    visible = (k_pos // CHUNK)[None, :] <= (q_pos // CHUNK)[:, None]
    scores = jnp.where(visible, scores, -jnp.inf)
    p = jax.nn.softmax(scores, axis=-1)
    a = p[:, :, 0] - lam * p[:, :, 1]
    return jnp.einsum('bhqk,bkhe->bqhe', a.astype(v.dtype), v)


def prompt_diff_attention(q, k, v, lam):
    b, t = q.shape[0], q.shape[1]
    nb = t // Q_BLOCK
    qb = q.reshape(b, nb, Q_BLOCK, N_HEADS, 2, D_QK).swapaxes(0, 1)
    pb = jnp.arange(t).reshape(nb, Q_BLOCK)
    k_pos = jnp.arange(t)

    def one_block(args):
        qi, pi = args
        return diff_attention(qi, k, v, pi, k_pos, lam)

    out = lax.map(one_block, (qb, pb))
    return out.swapaxes(0, 1).reshape(b, t, N_HEADS, D_V)


def causal_depthwise_conv(xp, w, b):
    y = lax.conv_general_dilated(xp, w[:, None, :].astype(xp.dtype), window_strides=(1,), padding='VALID',
                                 dimension_numbers=('NWC', 'WIO', 'NWC'), feature_group_count=CONV_CH)
    return y + b


def hybrid_layer(x, pos, past_k, past_v, past_conv, lam_init,
                 g_pre_mix, w_in, lq1, lk1, lq2, lk2, g_subln, w_attn_out,
                 w_dw, b_dw, g_cn, b_cn, w_conv_out, b_conv_out, w_o, g_post_mix,
                 g_pre_ffn, w_ffn_gate, w_ffn_up, w_ffn_down, g_post_ffn):
    b, t, _ = x.shape
    h = rms_norm(x, g_pre_mix)
    z = h @ w_in
    o1 = Q_COLS
    o2 = o1 + K_COLS
    o3 = o2 + V_COLS
    o4 = o3 + CONV_COLS
    q = partial_rope(z[..., :o1].reshape(b, t, N_HEADS, 2, D_QK), pos)
    k = partial_rope(z[..., o1:o2].reshape(b, t, N_HEADS, 2, D_QK), pos)
    v = z[..., o2:o3].reshape(b, t, N_HEADS, D_V)
    u = z[..., o3:o4]
    gate = z[..., o4:]

    lam = (jnp.exp(jnp.sum(lq1.astype(jnp.float32) * lk1.astype(jnp.float32)))
           - jnp.exp(jnp.sum(lq2.astype(jnp.float32) * lk2.astype(jnp.float32))) + lam_init)
    if past_k is None:
        o = prompt_diff_attention(q, k, v, lam)
    else:
        k_all = jnp.concatenate([past_k, k], axis=1)
        v_all = jnp.concatenate([past_v, v], axis=1)
        o = diff_attention(q, k_all, v_all, pos, jnp.arange(k_all.shape[1]), lam)
    o = rms_norm(o, g_subln) * (1.0 - lam_init)
    a_branch = o.reshape(b, t, ATTN_WIDTH) @ w_attn_out

    glu = u[..., :CONV_CH] * jax.nn.sigmoid(u[..., CONV_CH:])
    if past_conv is None:
        past_conv = jnp.zeros((b, CONV_STATE, CONV_CH), glu.dtype)
    cp = jnp.concatenate([past_conv.astype(glu.dtype), glu], axis=1)
    c = causal_depthwise_conv(cp, w_dw, b_dw)
    c = jax.nn.silu(layer_norm(c, g_cn, b_cn))
    c_branch = c @ w_conv_out + b_conv_out
    new_conv = cp[:, -CONV_STATE:]

    merged = jax.nn.sigmoid(gate[..., :D_MODEL]) * a_branch + jax.nn.sigmoid(gate[..., D_MODEL:]) * c_branch
    x = x + rms_norm(merged @ w_o, g_post_mix)

    h2 = rms_norm(x, g_pre_ffn)
    f = (jax.nn.silu(h2 @ w_ffn_gate) * (h2 @ w_ffn_up)) @ w_ffn_down
    x = x + rms_norm(f, g_post_ffn)
    return x, k, v, new_conv


def setup_inputs(seed: int = 0) -> dict:
    key = jax.random.key(seed)
    ks = jax.random.split(key, 32)
    f32 = jnp.float32

    def nrm(k, shape, scale):
        return jax.random.normal(k, shape, f32) * scale

    def gain(k, shape):
        return 1.0 + 0.02 * jax.random.normal(k, shape, f32)

    L = DEPTH
    return {
        "x_prompt": nrm(ks[0], (BATCH, SEQ, D_MODEL), 1.0),
        "x_sample": nrm(ks[1], (DEC_BATCH, DEC_SEQ, D_MODEL), 1.0),
        "cache_k": nrm(ks[2], (L, DEC_BATCH, PAST_LEN, N_HEADS, 2, D_QK), 1.0),
        "cache_v": nrm(ks[3], (L, DEC_BATCH, PAST_LEN, N_HEADS, D_V), 1.0),
        "state_conv": nrm(ks[4], (L, DEC_BATCH, CONV_STATE, CONV_CH), 0.5),
        "g_pre_mix": gain(ks[5], (L, D_MODEL)),
        "w_in": nrm(ks[6], (L, D_MODEL, IN_COLS), D_MODEL ** -0.5),
        "lambda_q1": nrm(ks[7], (L, D_QK), 0.1),
        "lambda_k1": nrm(ks[8], (L, D_QK), 0.1),
        "lambda_q2": nrm(ks[9], (L, D_QK), 0.1),
        "lambda_k2": nrm(ks[10], (L, D_QK), 0.1),
        "g_subln": gain(ks[11], (L, D_V)),
        "w_attn_out": nrm(ks[12], (L, ATTN_WIDTH, D_MODEL), ATTN_WIDTH ** -0.5),
        "w_dw": nrm(ks[13], (L, CONV_WIDTH, CONV_CH), CONV_WIDTH ** -0.5),
        "b_dw": nrm(ks[14], (L, CONV_CH), 0.02),
        "g_conv_norm": gain(ks[15], (L, CONV_CH)),
        "b_conv_norm": nrm(ks[16], (L, CONV_CH), 0.02),
        "w_conv_out": nrm(ks[17], (L, CONV_CH, D_MODEL), CONV_CH ** -0.5),
        "b_conv_out": nrm(ks[18], (L, D_MODEL), 0.02),
        "w_o": nrm(ks[19], (L, D_MODEL, D_MODEL), D_MODEL ** -0.5),
        "g_post_mix": gain(ks[20], (L, D_MODEL)),
        "g_pre_ffn": gain(ks[21], (L, D_MODEL)),
        "w_ffn_gate": nrm(ks[22], (L, D_MODEL, D_FF), D_MODEL ** -0.5),
        "w_ffn_up": nrm(ks[23], (L, D_MODEL, D_FF), D_MODEL ** -0.5),
        "w_ffn_down": nrm(ks[24], (L, D_FF, D_MODEL), D_FF ** -0.5),
        "g_post_ffn": gain(ks[25], (L, D_MODEL)),
    }


def reference(x_prompt, x_sample, cache_k, cache_v, state_conv,
              g_pre_mix, w_in, lambda_q1, lambda_k1, lambda_q2, lambda_k2, g_subln, w_attn_out,
              w_dw, b_dw, g_conv_norm, b_conv_norm, w_conv_out, b_conv_out, w_o, g_post_mix,
              g_pre_ffn, w_ffn_gate, w_ffn_up, w_ffn_down, g_post_ffn):
    pos_p = jnp.arange(x_prompt.shape[1])
    pos_s = cache_k.shape[2] + jnp.arange(x_sample.shape[1])
    yp = x_prompt
    ys = x_sample
    kp_l, vp_l, cp_l, ks_l, vs_l, cs_l = [], [], [], [], [], []
    for l in range(DEPTH):
        w = (g_pre_mix[l], w_in[l], lambda_q1[l], lambda_k1[l], lambda_q2[l], lambda_k2[l], g_subln[l],
             w_attn_out[l], w_dw[l], b_dw[l], g_conv_norm[l], b_conv_norm[l], w_conv_out[l], b_conv_out[l],
             w_o[l], g_post_mix[l], g_pre_ffn[l], w_ffn_gate[l], w_ffn_up[l], w_ffn_down[l], g_post_ffn[l])
        li = lambda_init(l)
        yp, kp, vp, cp = hybrid_layer(yp, pos_p, None, None, None, li, *w)
        ys, kn, vn, cn = hybrid_layer(ys, pos_s, cache_k[l], cache_v[l], state_conv[l], li, *w)
        kp_l.append(kp)
        vp_l.append(vp)
        cp_l.append(cp)
        ks_l.append(kn)
        vs_l.append(vn)
        cs_l.append(cn)
    k_prompt = jnp.stack(kp_l)
    v_prompt = jnp.stack(vp_l)
    conv_prompt = jnp.stack(cp_l)
    k_sample = jnp.stack(ks_l)
    v_sample = jnp.stack(vs_l)
    conv_sample = jnp.stack(cs_l)
    return (yp, ys, k_prompt, v_prompt, conv_prompt, k_sample, v_sample, conv_sample)
```

```python
import functools
import math

import jax
import jax.numpy as jnp
from jax import lax
from jax.experimental import pallas as pl
from jax.experimental.pallas import tpu as pltpu

F32 = jnp.float32
BF16 = jnp.bfloat16

D_MODEL = 2048
CHUNK = 64
N_HEADS = 8
D_QK = 64
D_V = 2 * D_QK
HEAD_COLS = 2 * D_QK
ATTN_WIDTH = N_HEADS * D_V
CONV_CH = 1024
CONV_WIDTH = 31
CONV_STATE = CONV_WIDTH - 1
ROPE_DIM = D_QK // 4
ROPE_THETA = 500000.0
Q_COLS = N_HEADS * 2 * D_QK
QKVU_COLS = 3 * Q_COLS + 2 * CONV_CH
GATE_COL0 = QKVU_COLS
EPS = 1e-6
NEG_BIG = -1e30

LANES = 128
SUBLANES = 8
VMEM_LIMIT_BYTES = 56 * 1024 * 1024

ROW_TILE = 512
COL_TILE = 1024
MERGE_COL_TILE = 512
FFN_TILE = 512
ATTN_TILE = 512
CONV_TIME_TILE = 512
CONV_ROWS = 16
HALO_ROWS = 32
SAMPLE_KV_TILE = 1024
NEW_KEY_PAD = 128


def _lambda_init(layer):
    return 0.8 - 0.6 * math.exp(-0.3 * layer)


def _params(semantics):
    return pltpu.CompilerParams(dimension_semantics=semantics, vmem_limit_bytes=VMEM_LIMIT_BYTES)


def _rms_norm(x, g):
    y = x * lax.rsqrt(jnp.mean(x * x, axis=-1, keepdims=True) + EPS)
    return y * g


def _sigmoid(x):
    return 1.0 / (1.0 + jnp.exp(-x))


def _rope(z, cos_t, sin_lo, sin_hi):
    outs = []
    for blk in range(z.shape[1] // LANES):
        zb = z[:, blk * LANES:(blk + 1) * LANES]
        from_hi = pltpu.roll(zb, LANES - ROPE_DIM // 2, 1)
        from_lo = pltpu.roll(zb, ROPE_DIM // 2, 1)
        outs.append(zb * cos_t + from_hi * sin_lo + from_lo * sin_hi)
    return jnp.concatenate(outs, axis=1)


def _in_proj_kernel(x_ref, g_ref, w_ref, cos_ref, slo_ref, shi_ref,
                    q_ref, kf_ref, kb_ref, vf_ref, vb_ref, glu_ref, h_scr, u_scr):
    j = pl.program_id(1)

    @pl.when(j == 0)
    def _():
        h_scr[...] = _rms_norm(x_ref[...], g_ref[...]).astype(BF16)

    z = jnp.dot(h_scr[...], w_ref[...], preferred_element_type=F32)

    @pl.when(j == 0)
    def _():
        q = _rope(z, cos_ref[...], slo_ref[...], shi_ref[...]) * (D_QK ** -0.5)
        q_ref[...] = q.astype(BF16)

    @pl.when(j == 1)
    def _():
        k = _rope(z, cos_ref[...], slo_ref[...], shi_ref[...])
        kf_ref[...] = k
        kb_ref[...] = k.astype(BF16)

    @pl.when(j == 2)
    def _():
        vf_ref[...] = z
        vb_ref[...] = z.astype(BF16)

    @pl.when(j == 3)
    def _():
        u_scr[...] = z

    @pl.when(j == 4)
    def _():
        glu_ref[...] = u_scr[...] * _sigmoid(z)


def _in_proj(x, g, w_in_bf16, cos_t, sin_lo, sin_hi):
    n = x.shape[0]
    tm = ROW_TILE
    n_col = QKVU_COLS // COL_TILE
    row_blk = lambda i, j: (i, 0)
    wide = lambda dt: jax.ShapeDtypeStruct((n, Q_COLS), dt)
    out_spec = pl.BlockSpec((tm, Q_COLS), row_blk)
    return pl.pallas_call(
        _in_proj_kernel,
        grid=(n // tm, n_col),
        in_specs=[
            pl.BlockSpec((tm, D_MODEL), row_blk),
            pl.BlockSpec((1, D_MODEL), lambda i, j: (0, 0)),
            pl.BlockSpec((D_MODEL, COL_TILE), lambda i, j: (0, j)),
            pl.BlockSpec((tm, LANES), row_blk),
            pl.BlockSpec((tm, LANES), row_blk),
            pl.BlockSpec((tm, LANES), row_blk),
        ],
        out_specs=[out_spec] * 6,
        out_shape=[wide(BF16), wide(F32), wide(BF16), wide(F32), wide(BF16), wide(F32)],
        scratch_shapes=[pltpu.VMEM((tm, D_MODEL), BF16), pltpu.VMEM((tm, COL_TILE), F32)],
        compiler_params=_params(("parallel", "arbitrary")),
        name="in_proj",
    )(x, g, w_in_bf16, cos_t, sin_lo, sin_hi)


def _rope_tables(pos):
    half = ROPE_DIM // 2
    inv_freq = ROPE_THETA ** (-jnp.arange(0, ROPE_DIM, 2, dtype=F32) / ROPE_DIM)
    ang = pos.astype(F32)[:, None] * inv_freq[None, :]
    cos, sin = jnp.cos(ang), jnp.sin(ang)
    ones = jnp.ones((pos.shape[0], D_QK - ROPE_DIM), F32)
    zeros = jnp.zeros((pos.shape[0], D_QK - half), F32)
    cos_t = jnp.concatenate([cos, cos, ones], axis=1)
    sin_lo = jnp.concatenate([-sin, zeros], axis=1)
    sin_hi = jnp.concatenate([zeros[:, :half], sin, zeros[:, :D_QK - ROPE_DIM]], axis=1)
    two = lambda t: jnp.concatenate([t, t], axis=1)
    return two(cos_t), two(sin_lo), two(sin_hi)


def _conv_kernel(cur_ref, halo_ref, state_ref, wdw_ref, bdw_ref, g_ref, b_ref, out_ref, buf):
    tt = cur_ref.shape[0]
    ti = pl.program_id(1)

    @pl.when(ti == 0)
    def _():
        buf[0:HALO_ROWS, :] = state_ref[...]

    @pl.when(ti > 0)
    def _():
        buf[0:HALO_ROWS, :] = halo_ref[...]

    buf[HALO_ROWS:HALO_ROWS + tt, :] = cur_ref[...]
    first = HALO_ROWS - CONV_STATE

    def chunk(r, carry):
        r0 = pl.multiple_of(r * CONV_ROWS, CONV_ROWS)
        win = buf[pl.ds(r0, CONV_ROWS + HALO_ROWS), :]
        acc = jnp.zeros((CONV_ROWS, CONV_CH), F32)
        for res in range(SUBLANES):
            shifted = win if res == 0 else win[res:res + CONV_ROWS + HALO_ROWS - SUBLANES, :]
            for a in range((CONV_ROWS + HALO_ROWS) // SUBLANES):
                w = a * SUBLANES + res - first
                if 0 <= w < CONV_WIDTH:
                    rows = shifted[a * SUBLANES:a * SUBLANES + CONV_ROWS, :]
                    acc = acc + rows * wdw_ref[w:w + 1, :]
        acc = acc + bdw_ref[...]
        mu = jnp.mean(acc, axis=-1, keepdims=True)
        d = acc - mu
        var = jnp.mean(d * d, axis=-1, keepdims=True)
        y = d * lax.rsqrt(var + EPS) * g_ref[...] + b_ref[...]
        out_ref[pl.ds(r0, CONV_ROWS), :] = (y * _sigmoid(y)).astype(BF16)
        return carry

    lax.fori_loop(0, tt // CONV_ROWS, chunk, 0)


def _conv_module(glu, state, w_dw, b_dw, g_cn, b_cn, batch, seq):
    tt = min(CONV_TIME_TILE, seq)
    glu3 = glu.reshape(batch, seq, CONV_CH)
    halo_per_tile = tt // HALO_ROWS
    vec = pl.BlockSpec((1, CONV_CH), lambda b, t: (0, 0))
    out = pl.pallas_call(
        _conv_kernel,
        grid=(batch, seq // tt),
        in_specs=[
            pl.BlockSpec((None, tt, CONV_CH), lambda b, t: (b, t, 0)),
            pl.BlockSpec((None, HALO_ROWS, CONV_CH),
                         lambda b, t: (b, jnp.maximum(t * halo_per_tile - 1, 0), 0)),
            pl.BlockSpec((None, HALO_ROWS, CONV_CH), lambda b, t: (b, 0, 0)),
            pl.BlockSpec((HALO_ROWS, CONV_CH), lambda b, t: (0, 0)),
            vec, vec, vec,
        ],
        out_specs=pl.BlockSpec((None, tt, CONV_CH), lambda b, t: (b, t, 0)),
        out_shape=jax.ShapeDtypeStruct((batch, seq, CONV_CH), BF16),
        scratch_shapes=[pltpu.VMEM((HALO_ROWS + tt, CONV_CH), F32)],
        compiler_params=_params(("parallel", "parallel")),
        name="conv_module",
    )(glu3, glu3, state, w_dw, b_dw, g_cn, b_cn)
    return out.reshape(batch * seq, CONV_CH)


def _lambda_value(lq1_ref, lk1_ref, lq2_ref, lk2_ref, lam_init):
    s1 = jnp.sum(lq1_ref[...] * lk1_ref[...], axis=-1, keepdims=True)
    s2 = jnp.sum(lq2_ref[...] * lk2_ref[...], axis=-1, keepdims=True)
    return jnp.exp(s1) - jnp.exp(s2) + lam_init


def _softmax_step(s, v_bf16, m_scr, l_scr, acc_scr):
    m_prev = m_scr[...]
    m_next = jnp.maximum(m_prev, jnp.max(s, axis=1, keepdims=True))
    alpha = jnp.exp(m_prev - m_next)
    p = jnp.exp(s - m_next[:, 0:1])
    l_scr[...] = alpha * l_scr[...] + jnp.sum(p, axis=1, keepdims=True)
    pv = jnp.dot(p.astype(BF16), v_bf16, preferred_element_type=F32)
    if acc_scr.shape[1] == LANES:
        acc_scr[...] = acc_scr[...] * alpha + pv
    else:
        acc_scr[...] = acc_scr[...] * alpha[:, 0:1] + pv
    m_scr[...] = m_next


def _head_output(o1, o2, lam, g_sub, lam_init):
    o = o1 - lam * o2
    return _rms_norm(o, g_sub) * (1.0 - lam_init)


def _attn_prompt_kernel(lq1_ref, lk1_ref, lq2_ref, lk2_ref, gsub_ref, q_ref, k_ref, v_ref,
                        o_ref, qs_scr, m_scr, l_scr, acc_scr, *, lam_init):
    tq = q_ref.shape[0]
    tk = tq
    qi = pl.program_id(2)

    q = q_ref[...]
    lane = lax.broadcasted_iota(jnp.int32, q.shape, 1)
    zero = jnp.zeros_like(q)
    qs_scr[0:tq, :] = jnp.where(lane < D_QK, q, zero)
    qs_scr[tq:2 * tq, :] = jnp.where(lane >= D_QK, q, zero)
    m_scr[...] = jnp.full(m_scr.shape, NEG_BIG, F32)
    l_scr[...] = jnp.zeros(l_scr.shape, F32)
    acc_scr[...] = jnp.zeros(acc_scr.shape, F32)

    def scores(c):
        start = pl.multiple_of(c * tk, tk)
        kc = k_ref[pl.ds(start, tk), :]
        s = lax.dot_general(qs_scr[...], kc, (((1,), (1,)), ((), ())), preferred_element_type=F32)
        return s, v_ref[pl.ds(start, tk), :]

    def full_step(c, carry):
        s, vc = scores(c)
        _softmax_step(s, vc, m_scr, l_scr, acc_scr)
        return carry

    lax.fori_loop(0, qi, full_step, 0)

    s, vc = scores(qi)
    row = lax.broadcasted_iota(jnp.int32, s.shape, 0)
    col = lax.broadcasted_iota(jnp.int32, s.shape, 1)
    chunk_bits = CHUNK.bit_length() - 1
    visible = (col >> chunk_bits) <= ((row & (tq - 1)) >> chunk_bits)
    _softmax_step(jnp.where(visible, s, NEG_BIG), vc, m_scr, l_scr, acc_scr)

    lam = _lambda_value(lq1_ref, lk1_ref, lq2_ref, lk2_ref, lam_init)
    o1 = acc_scr[0:tq, :] / l_scr[0:tq, :]
    o2 = acc_scr[tq:2 * tq, :] / l_scr[tq:2 * tq, :]
    o_ref[...] = _head_output(o1, o2, lam, gsub_ref[...], lam_init).astype(BF16)


def _attn_prompt(q, k, v, lam_vecs, g_sub, batch, seq, lam_init):
    tq = ATTN_TILE
    n_q = seq // tq
    vec64 = pl.BlockSpec((1, D_QK), lambda b, h, i: (0, 0))
    kv_spec = pl.BlockSpec((seq, HEAD_COLS), lambda b, h, i: (b, h))
    tile_spec = pl.BlockSpec((tq, HEAD_COLS), lambda b, h, i: (b * n_q + i, h))
    return pl.pallas_call(
        functools.partial(_attn_prompt_kernel, lam_init=lam_init),
        grid=(batch, N_HEADS, n_q),
        in_specs=[vec64, vec64, vec64, vec64,
                  pl.BlockSpec((1, D_V), lambda b, h, i: (0, 0)),
                  tile_spec, kv_spec, kv_spec],
        out_specs=tile_spec,
        out_shape=jax.ShapeDtypeStruct((batch * seq, ATTN_WIDTH), BF16),
        scratch_shapes=[pltpu.VMEM((2 * tq, HEAD_COLS), BF16),
                        pltpu.VMEM((2 * tq, LANES), F32),
                        pltpu.VMEM((2 * tq, LANES), F32),
                        pltpu.VMEM((2 * tq, D_V), F32)],
        compiler_params=_params(("parallel", "parallel", "arbitrary")),
        name="attn_prompt",
    )(*lam_vecs, g_sub, q, k, v)


def _attn_sample_kernel(lq1_ref, lk1_ref, lq2_ref, lk2_ref, gsub_ref, q_ref, ck_ref, cv_ref,
                        kn_ref, vn_ref, o_ref, qs_scr, kpad_scr, vpad_scr, m_scr, l_scr, acc_scr,
                        *, lam_init, n_cache_tiles):
    t = q_ref.shape[0]
    n_maps = 2 * N_HEADS
    j = pl.program_id(1)

    @pl.when(j == 0)
    def _():
        q = q_ref[...]
        lane = lax.broadcasted_iota(jnp.int32, q.shape, 1)
        zero = jnp.zeros_like(q)
        map_bits = D_QK.bit_length() - 1
        for hm in range(n_maps):
            qs_scr[hm * t:(hm + 1) * t, :] = jnp.where((lane >> map_bits) == hm, q, zero)
        m_scr[...] = jnp.full(m_scr.shape, NEG_BIG, F32)
        l_scr[...] = jnp.zeros(l_scr.shape, F32)
        acc_scr[...] = jnp.zeros(acc_scr.shape, F32)

    @pl.when(j < n_cache_tiles)
    def _():
        kc = ck_ref[...].astype(BF16)
        s = lax.dot_general(qs_scr[...], kc, (((1,), (1,)), ((), ())), preferred_element_type=F32)
        _softmax_step(s, cv_ref[...].astype(BF16), m_scr, l_scr, acc_scr)

    @pl.when(j == n_cache_tiles)
    def _():
        kpad_scr[...] = jnp.zeros(kpad_scr.shape, BF16)
        vpad_scr[...] = jnp.zeros(vpad_scr.shape, BF16)
        kpad_scr[0:t, :] = kn_ref[...]
        vpad_scr[0:t, :] = vn_ref[...]
        s = lax.dot_general(qs_scr[...], kpad_scr[...], (((1,), (1,)), ((), ())),
                            preferred_element_type=F32)
        col = lax.broadcasted_iota(jnp.int32, s.shape, 1)
        _softmax_step(jnp.where(col < t, s, NEG_BIG), vpad_scr[...], m_scr, l_scr, acc_scr)

        lam = _lambda_value(lq1_ref, lk1_ref, lq2_ref, lk2_ref, lam_init)
        for h in range(N_HEADS):
            r1, r2 = (2 * h) * t, (2 * h + 1) * t
            cols = slice(h * D_V, (h + 1) * D_V)
            o1 = acc_scr[r1:r1 + t, cols] / l_scr[r1:r1 + t, :]
            o2 = acc_scr[r2:r2 + t, cols] / l_scr[r2:r2 + t, :]
            o_ref[:, cols] = _head_output(o1, o2, lam, gsub_ref[...], lam_init).astype(BF16)


def _attn_sample(q, k_new, v_new, cache_k, cache_v, lam_vecs, g_sub, lam_init):
    batch, past, _ = cache_k.shape
    t = q.shape[0] // batch
    n_tiles = past // SAMPLE_KV_TILE
    rows = 2 * N_HEADS * t
    vec64 = pl.BlockSpec((1, D_QK), lambda b, j: (0, 0))
    new_spec = pl.BlockSpec((t, Q_COLS), lambda b, j: (b, 0))
    cache_spec = pl.BlockSpec((None, SAMPLE_KV_TILE, Q_COLS),
                              lambda b, j: (b, jnp.minimum(j, n_tiles - 1), 0))
    return pl.pallas_call(
        functools.partial(_attn_sample_kernel, lam_init=lam_init, n_cache_tiles=n_tiles),
        grid=(batch, n_tiles + 1),
        in_specs=[vec64, vec64, vec64, vec64,
                  pl.BlockSpec((1, D_V), lambda b, j: (0, 0)),
                  new_spec, cache_spec, cache_spec, new_spec, new_spec],
        out_specs=new_spec,
        out_shape=jax.ShapeDtypeStruct((batch * t, ATTN_WIDTH), BF16),
        scratch_shapes=[pltpu.VMEM((rows, Q_COLS), BF16),
                        pltpu.VMEM((NEW_KEY_PAD, Q_COLS), BF16),
                        pltpu.VMEM((NEW_KEY_PAD, Q_COLS), BF16),
                        pltpu.VMEM((rows, LANES), F32),
                        pltpu.VMEM((rows, LANES), F32),
                        pltpu.VMEM((rows, ATTN_WIDTH), F32)],
        compiler_params=_params(("parallel", "arbitrary")),
        name="attn_sample",
    )(*lam_vecs, g_sub, q, cache_k, cache_v, k_new, v_new)


def _merge_kernel(x_ref, gpre_ref, wga_ref, wgb_ref, a_ref, c_ref, wao_ref, wco_ref, bco_ref,
                  wo_ref, gpost_ref, out_ref, h_scr, merged_scr, mix_scr, *, n_chunks):
    j = pl.program_id(1)

    @pl.when(j == 0)
    def _():
        h_scr[...] = _rms_norm(x_ref[...], gpre_ref[...]).astype(BF16)

    @pl.when(j < n_chunks)
    def _():
        h = h_scr[...]
        gate_a = jnp.dot(h, wga_ref[...], preferred_element_type=F32)
        gate_b = jnp.dot(h, wgb_ref[...], preferred_element_type=F32)
        a_branch = jnp.dot(a_ref[...], wao_ref[...], preferred_element_type=F32)
        c_branch = jnp.dot(c_ref[...], wco_ref[...], preferred_element_type=F32) + bco_ref[...]
        merged = _sigmoid(gate_a) * a_branch + _sigmoid(gate_b) * c_branch
        merged_scr[j] = merged.astype(BF16)

    @pl.when(j >= n_chunks)
    def _():
        merged = jnp.concatenate([merged_scr[c] for c in range(n_chunks)], axis=1)
        mix_scr[j - n_chunks] = jnp.dot(merged, wo_ref[...], preferred_element_type=F32)

    @pl.when(j == 2 * n_chunks - 1)
    def _():
        mix = jnp.concatenate([mix_scr[c] for c in range(n_chunks)], axis=1)
        out_ref[...] = x_ref[...] + _rms_norm(mix, gpost_ref[...])


def _merge(x, g_pre, w_in_bf16, a_heads, c_act, w_ao, w_co, b_co, w_o, g_post):
    n = x.shape[0]
    tm, tn = ROW_TILE, MERGE_COL_TILE
    nc = D_MODEL // tn
    gate_blk0 = GATE_COL0 // tn
    first = lambda i, j: (0, jnp.minimum(j, nc - 1))
    row_blk = lambda i, j: (i, 0)
    vec = pl.BlockSpec((1, D_MODEL), lambda i, j: (0, 0))
    return pl.pallas_call(
        functools.partial(_merge_kernel, n_chunks=nc),
        grid=(n // tm, 2 * nc),
        in_specs=[
            pl.BlockSpec((tm, D_MODEL), row_blk),
            vec,
            pl.BlockSpec((D_MODEL, tn), lambda i, j: (0, gate_blk0 + jnp.minimum(j, nc - 1))),
            pl.BlockSpec((D_MODEL, tn), lambda i, j: (0, gate_blk0 + nc + jnp.minimum(j, nc - 1))),
            pl.BlockSpec((tm, ATTN_WIDTH), row_blk),
            pl.BlockSpec((tm, CONV_CH), row_blk),
            pl.BlockSpec((ATTN_WIDTH, tn), first),
            pl.BlockSpec((CONV_CH, tn), first),
            pl.BlockSpec((1, tn), first),
            pl.BlockSpec((D_MODEL, tn), lambda i, j: (0, jnp.clip(j - nc, 0, nc - 1))),
            vec,
        ],
        out_specs=pl.BlockSpec((tm, D_MODEL), row_blk),
        out_shape=jax.ShapeDtypeStruct((n, D_MODEL), F32),
        scratch_shapes=[pltpu.VMEM((tm, D_MODEL), BF16),
                        pltpu.VMEM((nc, tm, tn), BF16),
                        pltpu.VMEM((nc, tm, tn), F32)],
        compiler_params=_params(("parallel", "arbitrary")),
        name="merge",
    )(x, g_pre, w_in_bf16, w_in_bf16, a_heads, c_act, w_ao, w_co, b_co, w_o, g_post)


def _ffn_kernel(x_ref, gpre_ref, wg_ref, wu_ref, wd_ref, gpost_ref, out_ref, h_scr, acc_scr):
    j = pl.program_id(1)

    @pl.when(j == 0)
    def _():
        h_scr[...] = _rms_norm(x_ref[...], gpre_ref[...]).astype(BF16)
        acc_scr[...] = jnp.zeros(acc_scr.shape, F32)

    h = h_scr[...]
    a = jnp.dot(h, wg_ref[...], preferred_element_type=F32)
    u = jnp.dot(h, wu_ref[...], preferred_element_type=F32)
    t = (a * _sigmoid(a)) * u
    acc_scr[...] += jnp.dot(t.astype(BF16), wd_ref[...], preferred_element_type=F32)

    @pl.when(j == pl.num_programs(1) - 1)
    def _():
        out_ref[...] = x_ref[...] + _rms_norm(acc_scr[...], gpost_ref[...])


def _ffn(x, g_pre, w_gate, w_up, w_down, g_post):
    n = x.shape[0]
    d_ff = w_gate.shape[1]
    tm, tf = ROW_TILE, FFN_TILE
    row_blk = lambda i, j: (i, 0)
    vec = pl.BlockSpec((1, D_MODEL), lambda i, j: (0, 0))
    return pl.pallas_call(
        _ffn_kernel,
        grid=(n // tm, d_ff // tf),
        in_specs=[
            pl.BlockSpec((tm, D_MODEL), row_blk),
            vec,
            pl.BlockSpec((D_MODEL, tf), lambda i, j: (0, j)),
            pl.BlockSpec((D_MODEL, tf), lambda i, j: (0, j)),
            pl.BlockSpec((tf, D_MODEL), lambda i, j: (j, 0)),
            vec,
        ],
        out_specs=pl.BlockSpec((tm, D_MODEL), row_blk),
        out_shape=jax.ShapeDtypeStruct((n, D_MODEL), F32),
        scratch_shapes=[pltpu.VMEM((tm, D_MODEL), BF16), pltpu.VMEM((tm, D_MODEL), F32)],
        compiler_params=_params(("parallel", "arbitrary")),
        name="ffn",
    )(x, g_pre, w_gate, w_up, w_down, g_post)


def _layer(x, pos, past_k, past_v, conv_state, lam_init, wts):
    batch, seq, _ = x.shape
    n = batch * seq
    x2 = x.reshape(n, D_MODEL)
    row = lambda v: v.reshape(1, -1)

    cos_t, sin_lo, sin_hi = _rope_tables(pos)
    if batch > 1:
        cos_t, sin_lo, sin_hi = (jnp.tile(t, (batch, 1)) for t in (cos_t, sin_lo, sin_hi))
    q, k_f32, k_bf16, v_f32, v_bf16, glu = _in_proj(
        x2, row(wts["g_pre_mix"]), wts["w_in"], cos_t, sin_lo, sin_hi)

    lam_vecs = [row(wts[name]) for name in ("lambda_q1", "lambda_k1", "lambda_q2", "lambda_k2")]
    g_sub = row(wts["g_subln"])
    if past_k is None:
        a_heads = _attn_prompt(q, k_bf16, v_bf16, lam_vecs, g_sub, batch, seq, lam_init)
        state = jnp.zeros((batch, HALO_ROWS, CONV_CH), F32)
    else:
        past = past_k.shape[1]
        assert (past + seq - 1) // CHUNK <= past // CHUNK and past % SAMPLE_KV_TILE == 0, \
            "sample path assumes every cached and new key is visible to every new query"
        a_heads = _attn_sample(q, k_bf16, v_bf16, past_k.reshape(batch, past, Q_COLS),
                               past_v.reshape(batch, past, ATTN_WIDTH), lam_vecs, g_sub, lam_init)
        state = jnp.pad(conv_state, ((0, 0), (HALO_ROWS - CONV_STATE, 0), (0, 0)))

    w_dw = jnp.pad(wts["w_dw"], ((0, HALO_ROWS - CONV_WIDTH), (0, 0)))
    c_act = _conv_module(glu, state, w_dw, row(wts["b_dw"]), row(wts["g_conv_norm"]),
                         row(wts["b_conv_norm"]), batch, seq)

    x_mid = _merge(x2, row(wts["g_pre_mix"]), wts["w_in"], a_heads, c_act, wts["w_attn_out"],
                   wts["w_conv_out"], row(wts["b_conv_out"]), wts["w_o"], row(wts["g_post_mix"]))
    y = _ffn(x_mid, row(wts["g_pre_ffn"]), wts["w_ffn_gate"], wts["w_ffn_up"], wts["w_ffn_down"],
             row(wts["g_post_ffn"]))

    glu3 = glu.reshape(batch, seq, CONV_CH)
    if seq >= CONV_STATE:
        new_conv = glu3[:, seq - CONV_STATE:]
    else:
        new_conv = jnp.concatenate([conv_state[:, seq:], glu3], axis=1)
    return (y.reshape(batch, seq, D_MODEL),
            k_f32.reshape(batch, seq, N_HEADS, 2, D_QK),
            v_f32.reshape(batch, seq, N_HEADS, D_V),
            new_conv)


def kernel(x_prompt, x_sample, cache_k, cache_v, state_conv, g_pre_mix, w_in, lambda_q1, lambda_k1,
           lambda_q2, lambda_k2, g_subln, w_attn_out, w_dw, b_dw, g_conv_norm, b_conv_norm,
           w_conv_out, b_conv_out, w_o, g_post_mix, g_pre_ffn, w_ffn_gate, w_ffn_up, w_ffn_down,
           g_post_ffn):
    depth = w_in.shape[0]
    matmul_weights = dict(w_in=w_in, w_attn_out=w_attn_out, w_conv_out=w_conv_out, w_o=w_o,
                          w_ffn_gate=w_ffn_gate, w_ffn_up=w_ffn_up, w_ffn_down=w_ffn_down)
    other = dict(g_pre_mix=g_pre_mix, lambda_q1=lambda_q1, lambda_k1=lambda_k1, lambda_q2=lambda_q2,
                 lambda_k2=lambda_k2, g_subln=g_subln, w_dw=w_dw, b_dw=b_dw, g_conv_norm=g_conv_norm,
                 b_conv_norm=b_conv_norm, b_conv_out=b_conv_out, g_post_mix=g_post_mix,
                 g_pre_ffn=g_pre_ffn, g_post_ffn=g_post_ffn)
    pos_p = jnp.arange(x_prompt.shape[1])
    pos_s = cache_k.shape[2] + jnp.arange(x_sample.shape[1])
    yp, ys = x_prompt, x_sample
    outs = [[] for _ in range(6)]
    for l in range(depth):
        wts = {name: w[l].astype(BF16) for name, w in matmul_weights.items()}
        wts.update({name: w[l] for name, w in other.items()})
        li = _lambda_init(l)
        yp, kp, vp, cp = _layer(yp, pos_p, None, None, None, li, wts)
        ys, kn, vn, cn = _layer(ys, pos_s, cache_k[l], cache_v[l], state_conv[l], li, wts)
        for dst, val in zip(outs, (kp, vp, cp, kn, vn, cn)):
            dst.append(val)
    k_prompt, v_prompt, conv_prompt, k_sample, v_sample, conv_sample = (jnp.stack(o) for o in outs)
    return (yp, ys, k_prompt, v_prompt, conv_prompt, k_sample, v_sample, conv_sample)
```

```python
import functools
import math

import jax
import jax.numpy as jnp
from jax import lax
from jax.experimental import pallas as pl
from jax.experimental.pallas import tpu as pltpu

F32 = jnp.float32
BF16 = jnp.bfloat16

D_MODEL = 2048
CHUNK = 64
N_HEADS = 8
D_QK = 64
D_V = 2 * D_QK
HEAD_COLS = 2 * D_QK
ATTN_WIDTH = N_HEADS * D_V
CONV_CH = 1024
CONV_WIDTH = 31
CONV_STATE = CONV_WIDTH - 1
ROPE_DIM = D_QK // 4
ROPE_THETA = 500000.0
Q_COLS = N_HEADS * 2 * D_QK
QKVU_COLS = 3 * Q_COLS + 2 * CONV_CH
GATE_COL0 = QKVU_COLS
EPS = 1e-6
NEG_BIG = -1e30
Q_SCALE = (D_QK ** -0.5) * math.log2(math.e)

LANES = 128
SUBLANES = 8
VMEM_LIMIT_BYTES = 56 * 1024 * 1024

ROW_TILE = 512
COL_TILE = 1024
MERGE_COL_TILE = 512
FFN_TILE = 512
ATTN_TILE = 512
CONV_TIME_TILE = 512
CONV_ROWS = 32
HALO_ROWS = 32
SAMPLE_KV_TILE = 1024
NEW_KEY_PAD = 128


def _lambda_init(layer):
    return 0.8 - 0.6 * math.exp(-0.3 * layer)


def _params(semantics):
    return pltpu.CompilerParams(dimension_semantics=semantics, vmem_limit_bytes=VMEM_LIMIT_BYTES)


def _rms_norm(x, g):
    y = x * lax.rsqrt(jnp.mean(x * x, axis=-1, keepdims=True) + EPS)
    return y * g


def _sigmoid(x):
    return 1.0 / (1.0 + jnp.exp(-x))


def _rope(z, cos_t, sin_lo, sin_hi):
    outs = []
    for blk in range(z.shape[1] // LANES):
        zb = z[:, blk * LANES:(blk + 1) * LANES]
        from_hi = pltpu.roll(zb, LANES - ROPE_DIM // 2, 1)
        from_lo = pltpu.roll(zb, ROPE_DIM // 2, 1)
        outs.append(zb * cos_t + from_hi * sin_lo + from_lo * sin_hi)
    return jnp.concatenate(outs, axis=1)


def _in_proj_kernel(x_ref, g_ref, w_ref, cos_ref, slo_ref, shi_ref,
                    q_ref, kf_ref, kb_ref, vf_ref, vb_ref, glu_ref, h_scr, u_scr, *, transpose_k):
    j = pl.program_id(1)

    @pl.when(j == 0)
    def _():
        h_scr[...] = _rms_norm(x_ref[...], g_ref[...]).astype(BF16)

    z = jnp.dot(h_scr[...], w_ref[...], preferred_element_type=F32)

    @pl.when(j == 0)
    def _():
        q = _rope(z, cos_ref[...], slo_ref[...], shi_ref[...]) * Q_SCALE
        q_ref[...] = q.astype(BF16)

    @pl.when(j == 1)
    def _():
        k = _rope(z, cos_ref[...], slo_ref[...], shi_ref[...])
        if transpose_k:
            k = k.T.reshape(N_HEADS, HEAD_COLS, k.shape[0])
        kf_ref[...] = k
        kb_ref[...] = k.astype(BF16)

    @pl.when(j == 2)
    def _():
        vf_ref[...] = z
        vb_ref[...] = z.astype(BF16)

    @pl.when(j == 3)
    def _():
        u_scr[...] = z

    @pl.when(j == 4)
    def _():
        glu_ref[...] = u_scr[...] * _sigmoid(z)


def _in_proj(x, g, w_in_bf16, cos_t, sin_lo, sin_hi, seq_tiles=None):
    n = x.shape[0]
    tm = ROW_TILE
    n_col = QKVU_COLS // COL_TILE
    row_blk = lambda i, j: (i, 0)
    wide = lambda dt: jax.ShapeDtypeStruct((n, Q_COLS), dt)
    out_spec = pl.BlockSpec((tm, Q_COLS), row_blk)
    if seq_tiles is None:
        k_specs = [out_spec, out_spec]
        k_shapes = [wide(F32), wide(BF16)]
    else:
        batch = n // (seq_tiles * tm)
        k_specs = [pl.BlockSpec((None, N_HEADS, HEAD_COLS, tm),
                                lambda i, j: (i // seq_tiles, 0, 0, i % seq_tiles)),
                   pl.BlockSpec((None, N_HEADS, None, HEAD_COLS, tm),
                                lambda i, j: (i // seq_tiles, 0, i % seq_tiles, 0, 0))]
        k_shapes = [jax.ShapeDtypeStruct((batch, N_HEADS, HEAD_COLS, seq_tiles * tm), F32),
                    jax.ShapeDtypeStruct((batch, N_HEADS, seq_tiles, HEAD_COLS, tm), BF16)]
    return pl.pallas_call(
        functools.partial(_in_proj_kernel, transpose_k=seq_tiles is not None),
        grid=(n // tm, n_col),
        in_specs=[
            pl.BlockSpec((tm, D_MODEL), row_blk),
            pl.BlockSpec((1, D_MODEL), lambda i, j: (0, 0)),
            pl.BlockSpec((D_MODEL, COL_TILE), lambda i, j: (0, j)),
            pl.BlockSpec((tm, LANES), row_blk),
            pl.BlockSpec((tm, LANES), row_blk),
            pl.BlockSpec((tm, LANES), row_blk),
        ],
        out_specs=[out_spec] + k_specs + [out_spec] * 3,
        out_shape=[wide(BF16)] + k_shapes + [wide(F32), wide(BF16), wide(F32)],
        scratch_shapes=[pltpu.VMEM((tm, D_MODEL), BF16), pltpu.VMEM((tm, COL_TILE), F32)],
        compiler_params=_params(("parallel", "arbitrary")),
        name="in_proj",
    )(x, g, w_in_bf16, cos_t, sin_lo, sin_hi)


def _rope_tables(pos):
    half = ROPE_DIM // 2
    inv_freq = ROPE_THETA ** (-jnp.arange(0, ROPE_DIM, 2, dtype=F32) / ROPE_DIM)
    ang = pos.astype(F32)[:, None] * inv_freq[None, :]
    cos, sin = jnp.cos(ang), jnp.sin(ang)
    ones = jnp.ones((pos.shape[0], D_QK - ROPE_DIM), F32)
    zeros = jnp.zeros((pos.shape[0], D_QK - half), F32)
    cos_t = jnp.concatenate([cos, cos, ones], axis=1)
    sin_lo = jnp.concatenate([-sin, zeros], axis=1)
    sin_hi = jnp.concatenate([zeros[:, :half], sin, zeros[:, :D_QK - ROPE_DIM]], axis=1)
    two = lambda t: jnp.concatenate([t, t], axis=1)
    return two(cos_t), two(sin_lo), two(sin_hi)


def _conv_kernel(cur_ref, halo_ref, state_ref, wdw_ref, bdw_ref, g_ref, b_ref, out_ref, buf):
    tt = cur_ref.shape[0]
    ti = pl.program_id(1)

    @pl.when(ti == 0)
    def _():
        buf[0, 0:HALO_ROWS, :] = state_ref[...]

    @pl.when(ti > 0)
    def _():
        buf[0, 0:HALO_ROWS, :] = halo_ref[...]

    buf[0, HALO_ROWS:HALO_ROWS + tt, :] = cur_ref[...]
    first = HALO_ROWS - CONV_STATE
    span = HALO_ROWS + tt - SUBLANES
    for res in range(1, SUBLANES):
        buf[res, 0:span, :] = buf[0, res:res + span, :]

    rep = lambda v: jnp.concatenate([v] * (CONV_ROWS // SUBLANES), axis=0)

    def chunk(r, carry):
        r0 = pl.multiple_of(r * CONV_ROWS, CONV_ROWS)
        groups = CONV_ROWS // SUBLANES
        acc = [bdw_ref[...]] * groups
        for w in range(CONV_WIDTH):
            a, res = divmod(first + w, SUBLANES)
            rows = buf[res, pl.ds(r0 + a * SUBLANES, CONV_ROWS), :]
            tap = wdw_ref[w]
            acc = [acc[g] + rows[g * SUBLANES:(g + 1) * SUBLANES, :] * tap for g in range(groups)]
        acc = jnp.concatenate(acc, axis=0)
        mu = jnp.mean(acc, axis=-1, keepdims=True)
        d = acc - mu
        var = jnp.mean(d * d, axis=-1, keepdims=True)
        y = d * lax.rsqrt(var + EPS) * rep(g_ref[...]) + rep(b_ref[...])
        out_ref[pl.ds(r0, CONV_ROWS), :] = (y * _sigmoid(y)).astype(BF16)
        return carry

    lax.fori_loop(0, tt // CONV_ROWS, chunk, 0, unroll=2)


def _conv_module(glu, state, w_dw, b_dw, g_cn, b_cn, batch, seq):
    tt = min(CONV_TIME_TILE, seq)
    glu3 = glu.reshape(batch, seq, CONV_CH)
    halo_per_tile = tt // HALO_ROWS
    vec = pl.BlockSpec((SUBLANES, CONV_CH), lambda b, t: (0, 0))
    rep = lambda v: jnp.broadcast_to(v.reshape(-1, 1, CONV_CH), (v.size // CONV_CH, SUBLANES, CONV_CH))
    w_dw = rep(w_dw)
    b_dw, g_cn, b_cn = (rep(v)[0] for v in (b_dw, g_cn, b_cn))
    out = pl.pallas_call(
        _conv_kernel,
        grid=(batch, seq // tt),
        in_specs=[
            pl.BlockSpec((None, tt, CONV_CH), lambda b, t: (b, t, 0)),
            pl.BlockSpec((None, HALO_ROWS, CONV_CH),
                         lambda b, t: (b, jnp.maximum(t * halo_per_tile - 1, 0), 0)),
            pl.BlockSpec((None, HALO_ROWS, CONV_CH), lambda b, t: (b, 0, 0)),
            pl.BlockSpec((CONV_WIDTH, SUBLANES, CONV_CH), lambda b, t: (0, 0, 0)),
            vec, vec, vec,
        ],
        out_specs=pl.BlockSpec((None, tt, CONV_CH), lambda b, t: (b, t, 0)),
        out_shape=jax.ShapeDtypeStruct((batch, seq, CONV_CH), BF16),
        scratch_shapes=[pltpu.VMEM((SUBLANES, HALO_ROWS + tt, CONV_CH), F32)],
        compiler_params=_params(("parallel", "parallel")),
        name="conv_module",
    )(glu3, glu3, state, w_dw, b_dw, g_cn, b_cn)
    return out.reshape(batch * seq, CONV_CH)


def _lambda_value(lq1_ref, lk1_ref, lq2_ref, lk2_ref, lam_init):
    s1 = jnp.sum(lq1_ref[...] * lk1_ref[...], axis=-1, keepdims=True)
    s2 = jnp.sum(lq2_ref[...] * lk2_ref[...], axis=-1, keepdims=True)
    return jnp.exp(s1) - jnp.exp(s2) + lam_init


def _softmax_step(s, v_bf16, m_scr, l_scr, acc_scr):
    p, alpha = _softmax_stats(s, m_scr, l_scr)
    acc_scr[...] = acc_scr[...] * alpha + jnp.dot(p, v_bf16, preferred_element_type=F32)


def _softmax_stats(s, m_scr, l_scr):
    blocks = [s[:, i * LANES:(i + 1) * LANES] for i in range(s.shape[1] // LANES)]
    m_prev = m_scr[...]
    m_next = jnp.maximum(m_prev, jnp.max(_tree(jnp.maximum, blocks), axis=1, keepdims=True))
    alpha = jnp.exp2(m_prev - m_next)
    p_blocks = [jnp.exp2(b - m_next) for b in blocks]
    l_scr[...] = alpha * l_scr[...] + _tree(jnp.add, p_blocks)
    m_scr[...] = m_next
    return jnp.concatenate([pb.astype(BF16) for pb in p_blocks], axis=1), alpha


def _tree(op, xs):
    while len(xs) > 1:
        xs = [op(xs[i], xs[i + 1]) if i + 1 < len(xs) else xs[i] for i in range(0, len(xs), 2)]
    return xs[0]


def _row_sums(l_lanes):
    return jnp.sum(l_lanes, axis=1, keepdims=True)


def _head_output(o1, o2, lam, g_sub, lam_init):
    o = o1 - lam * o2
    return _rms_norm(o, g_sub) * (1.0 - lam_init)


def _stack_maps(q, qs_scr, t):
    lane = lax.broadcasted_iota(jnp.int32, q.shape, 1)
    zero = jnp.zeros_like(q)
    qs_scr[0:t, :] = jnp.where(lane < D_QK, q, zero)
    qs_scr[t:2 * t, :] = jnp.where(lane >= D_QK, q, zero)


def _attn_prompt_kernel(lq1_ref, lk1_ref, lq2_ref, lk2_ref, gsub_ref, q_ref, kt_ref, v_ref,
                        o_ref, qs_scr, s0_scr, s1_scr, m_scr, l_scr, acc_scr, *, lam_init):
    tq = q_ref.shape[0]
    tk = tq
    qi = pl.program_id(2)

    _stack_maps(q_ref[...], qs_scr, tq)
    m_scr[...] = jnp.full(m_scr.shape, NEG_BIG, F32)
    l_scr[...] = jnp.zeros(l_scr.shape, F32)
    acc_scr[...] = jnp.zeros(acc_scr.shape, F32)

    def scores(c, s_scr):
        s_scr[...] = jnp.dot(qs_scr[...], kt_ref[c], preferred_element_type=F32)

    def absorb(c, s_scr, mask=None):
        s = s_scr[...]
        if mask is not None:
            s = jnp.where(mask, s, NEG_BIG)
        vc = v_ref[pl.ds(pl.multiple_of(c * tk, tk), tk), :]
        p, alpha = _softmax_stats(s, m_scr, l_scr)
        zeros = jnp.zeros_like(vc)
        pv = (jnp.dot(p[0:tq], jnp.concatenate([vc, zeros], axis=1), preferred_element_type=F32)
              + jnp.dot(p[tq:2 * tq], jnp.concatenate([zeros, vc], axis=1),
                        preferred_element_type=F32))
        rescale = jnp.concatenate([alpha[0:tq], alpha[tq:2 * tq]], axis=1)
        acc_scr[...] = acc_scr[...] * rescale + pv

    odd = qi & 1

    @pl.when(odd == 1)
    def _():
        scores(0, s1_scr)
        absorb(0, s1_scr)

    scores(odd, s0_scr)

    def pair(p, carry):
        c0 = odd + 2 * p
        scores(c0 + 1, s1_scr)
        absorb(c0, s0_scr)
        scores(c0 + 2, s0_scr)
        absorb(c0 + 1, s1_scr)
        return carry

    lax.fori_loop(0, (qi - odd) // 2, pair, 0)

    row = lax.broadcasted_iota(jnp.int32, (2 * tq, tk), 0)
    col = lax.broadcasted_iota(jnp.int32, (2 * tq, tk), 1)
    chunk_bits = CHUNK.bit_length() - 1
    visible = (col >> chunk_bits) <= ((row & (tq - 1)) >> chunk_bits)
    absorb(qi, s0_scr, visible)

    lam = _lambda_value(lq1_ref, lk1_ref, lq2_ref, lk2_ref, lam_init)
    o1 = acc_scr[:, 0:D_V] / _row_sums(l_scr[0:tq, :])
    o2 = acc_scr[:, D_V:2 * D_V] / _row_sums(l_scr[tq:2 * tq, :])
    o_ref[...] = _head_output(o1, o2, lam, gsub_ref[...], lam_init).astype(BF16)


def _attn_prompt(q, kt, v, lam_vecs, g_sub, batch, seq, lam_init):
    tq = ATTN_TILE
    n_q = seq // tq
    assert kt.shape == (batch, N_HEADS, n_q, HEAD_COLS, tq)
    vec64 = pl.BlockSpec((1, D_QK), lambda b, h, i: (0, 0))
    tile_spec = pl.BlockSpec((tq, HEAD_COLS), lambda b, h, i: (b * n_q + i, h))
    return pl.pallas_call(
        functools.partial(_attn_prompt_kernel, lam_init=lam_init),
        grid=(batch, N_HEADS, n_q),
        in_specs=[vec64, vec64, vec64, vec64,
                  pl.BlockSpec((1, D_V), lambda b, h, i: (0, 0)),
                  tile_spec,
                  pl.BlockSpec((None, None, n_q, HEAD_COLS, tq), lambda b, h, i: (b, h, 0, 0, 0)),
                  pl.BlockSpec((seq, D_V), lambda b, h, i: (b, h))],
        out_specs=tile_spec,
        out_shape=jax.ShapeDtypeStruct((batch * seq, ATTN_WIDTH), BF16),
        scratch_shapes=[pltpu.VMEM((2 * tq, HEAD_COLS), BF16),
                        pltpu.VMEM((2 * tq, tq), F32),
                        pltpu.VMEM((2 * tq, tq), F32),
                        pltpu.VMEM((2 * tq, LANES), F32),
                        pltpu.VMEM((2 * tq, LANES), F32),
                        pltpu.VMEM((tq, 2 * D_V), F32)],
        compiler_params=_params(("parallel", "parallel", "arbitrary")),
        name="attn_prompt",
    )(*lam_vecs, g_sub, q, kt, v)


def _attn_sample_kernel(lq1_ref, lk1_ref, lq2_ref, lk2_ref, gsub_ref, q_ref, ckt_ref, cv_ref,
                        kn_ref, vn_ref, o_ref, qs_scr, kpad_scr, vpad_scr, m_scr, l_scr, acc_scr,
                        *, lam_init, n_cache_tiles):
    t = q_ref.shape[0]
    tk = ckt_ref.shape[2]
    j = pl.program_id(1)

    @pl.when(j == 0)
    def _():
        for h in range(N_HEADS):
            _stack_maps(q_ref[:, h * HEAD_COLS:(h + 1) * HEAD_COLS], qs_scr.at[h], t)
        m_scr[...] = jnp.full(m_scr.shape, NEG_BIG, F32)
        l_scr[...] = jnp.zeros(l_scr.shape, F32)
        acc_scr[...] = jnp.zeros(acc_scr.shape, F32)

    @pl.when(j < n_cache_tiles)
    def _():
        for h in range(N_HEADS):
            s = jnp.dot(qs_scr[h], ckt_ref[h].astype(BF16), preferred_element_type=F32)
            vh = cv_ref[pl.ds(h, tk, stride=N_HEADS), :].astype(BF16)
            _softmax_step(s, vh, m_scr.at[h], l_scr.at[h], acc_scr.at[h])

    @pl.when(j == n_cache_tiles)
    def _():
        kpad_scr[...] = jnp.zeros(kpad_scr.shape, BF16)
        vpad_scr[...] = jnp.zeros(vpad_scr.shape, BF16)
        kpad_scr[0:t, :] = kn_ref[...]
        vpad_scr[0:t, :] = vn_ref[...]
        lam = _lambda_value(lq1_ref, lk1_ref, lq2_ref, lk2_ref, lam_init)
        col = lax.broadcasted_iota(jnp.int32, (2 * t, NEW_KEY_PAD), 1)
        for h in range(N_HEADS):
            cols = slice(h * HEAD_COLS, (h + 1) * HEAD_COLS)
            s = lax.dot_general(qs_scr[h], kpad_scr[:, cols], (((1,), (1,)), ((), ())),
                                preferred_element_type=F32)
            _softmax_step(jnp.where(col < t, s, NEG_BIG), vpad_scr[:, cols],
                          m_scr.at[h], l_scr.at[h], acc_scr.at[h])
            o1 = acc_scr[h, 0:t, :] / _row_sums(l_scr[h, 0:t, :])
            o2 = acc_scr[h, t:2 * t, :] / _row_sums(l_scr[h, t:2 * t, :])
            o_ref[:, cols] = _head_output(o1, o2, lam, gsub_ref[...], lam_init).astype(BF16)


def _attn_sample(q, k_new, v_new, cache_kt, cache_v, lam_vecs, g_sub, lam_init):
    batch, _, _, past = cache_kt.shape
    t = q.shape[0] // batch
    tk = SAMPLE_KV_TILE
    n_tiles = past // tk
    vec64 = pl.BlockSpec((1, D_QK), lambda b, j: (0, 0))
    new_spec = pl.BlockSpec((t, Q_COLS), lambda b, j: (b, 0))
    tile = lambda j: jnp.minimum(j, n_tiles - 1)
    stats = pltpu.VMEM((N_HEADS, 2 * t, LANES), F32)
    return pl.pallas_call(
        functools.partial(_attn_sample_kernel, lam_init=lam_init, n_cache_tiles=n_tiles),
        grid=(batch, n_tiles + 1),
        in_specs=[vec64, vec64, vec64, vec64,
                  pl.BlockSpec((1, D_V), lambda b, j: (0, 0)),
                  new_spec,
                  pl.BlockSpec((None, N_HEADS, HEAD_COLS, tk), lambda b, j: (b, 0, 0, tile(j))),
                  pl.BlockSpec((None, tk * N_HEADS, D_V), lambda b, j: (b, tile(j), 0)),
                  new_spec, new_spec],
        out_specs=new_spec,
        out_shape=jax.ShapeDtypeStruct((batch * t, ATTN_WIDTH), BF16),
        scratch_shapes=[pltpu.VMEM((N_HEADS, 2 * t, HEAD_COLS), BF16),
                        pltpu.VMEM((NEW_KEY_PAD, Q_COLS), BF16),
                        pltpu.VMEM((NEW_KEY_PAD, ATTN_WIDTH), BF16),
                        stats, stats, stats],
        compiler_params=_params(("parallel", "arbitrary")),
        name="attn_sample",
    )(*lam_vecs, g_sub, q, cache_kt, cache_v, k_new, v_new)


def _merge_kernel(x_ref, gpre_ref, wga_ref, wgb_ref, a_ref, c_ref, wao_ref, wco_ref, bco_ref,
                  wo_ref, gpost_ref, out_ref, h_scr, merged_scr, mix_scr, *, n_chunks):
    j = pl.program_id(1)

    @pl.when(j == 0)
    def _():
        h_scr[...] = _rms_norm(x_ref[...], gpre_ref[...]).astype(BF16)

    @pl.when(j < n_chunks)
    def _():
        h = h_scr[...]
        gate_a = jnp.dot(h, wga_ref[...], preferred_element_type=F32)
        gate_b = jnp.dot(h, wgb_ref[...], preferred_element_type=F32)
        a_branch = jnp.dot(a_ref[...], wao_ref[...], preferred_element_type=F32)
        c_branch = jnp.dot(c_ref[...], wco_ref[...], preferred_element_type=F32) + bco_ref[...]
        merged = _sigmoid(gate_a) * a_branch + _sigmoid(gate_b) * c_branch
        merged_scr[j] = merged.astype(BF16)

    @pl.when(j >= n_chunks)
    def _():
        merged = jnp.concatenate([merged_scr[c] for c in range(n_chunks)], axis=1)
        mix_scr[j - n_chunks] = jnp.dot(merged, wo_ref[...], preferred_element_type=F32)

    @pl.when(j == 2 * n_chunks - 1)
    def _():
        mix = jnp.concatenate([mix_scr[c] for c in range(n_chunks)], axis=1)
        out_ref[...] = x_ref[...] + _rms_norm(mix, gpost_ref[...])


def _merge(x, g_pre, w_in_bf16, a_heads, c_act, w_ao, w_co, b_co, w_o, g_post):
    n = x.shape[0]
    tm, tn = ROW_TILE, MERGE_COL_TILE
    nc = D_MODEL // tn
    gate_blk0 = GATE_COL0 // tn
    first = lambda i, j: (0, jnp.minimum(j, nc - 1))
    row_blk = lambda i, j: (i, 0)
    vec = pl.BlockSpec((1, D_MODEL), lambda i, j: (0, 0))
    return pl.pallas_call(
        functools.partial(_merge_kernel, n_chunks=nc),
        grid=(n // tm, 2 * nc),
        in_specs=[
            pl.BlockSpec((tm, D_MODEL), row_blk),
            vec,
            pl.BlockSpec((D_MODEL, tn), lambda i, j: (0, gate_blk0 + jnp.minimum(j, nc - 1))),
            pl.BlockSpec((D_MODEL, tn), lambda i, j: (0, gate_blk0 + nc + jnp.minimum(j, nc - 1))),
            pl.BlockSpec((tm, ATTN_WIDTH), row_blk),
            pl.BlockSpec((tm, CONV_CH), row_blk),
            pl.BlockSpec((ATTN_WIDTH, tn), first),
            pl.BlockSpec((CONV_CH, tn), first),
            pl.BlockSpec((1, tn), first),
            pl.BlockSpec((D_MODEL, tn), lambda i, j: (0, jnp.clip(j - nc, 0, nc - 1))),
            vec,
        ],
        out_specs=pl.BlockSpec((tm, D_MODEL), row_blk),
        out_shape=jax.ShapeDtypeStruct((n, D_MODEL), F32),
        scratch_shapes=[pltpu.VMEM((tm, D_MODEL), BF16),
                        pltpu.VMEM((nc, tm, tn), BF16),
                        pltpu.VMEM((nc, tm, tn), F32)],
        compiler_params=_params(("parallel", "arbitrary")),
        name="merge",
    )(x, g_pre, w_in_bf16, w_in_bf16, a_heads, c_act, w_ao, w_co, b_co, w_o, g_post)


def _ffn_kernel(x_ref, gpre_ref, wg_ref, wu_ref, wd_ref, gpost_ref, out_ref, h_scr, acc_scr):
    j = pl.program_id(1)

    @pl.when(j == 0)
    def _():
        h_scr[...] = _rms_norm(x_ref[...], gpre_ref[...]).astype(BF16)
        acc_scr[...] = jnp.zeros(acc_scr.shape, F32)

    h = h_scr[...]
    a = jnp.dot(h, wg_ref[...], preferred_element_type=F32)
    u = jnp.dot(h, wu_ref[...], preferred_element_type=F32)
    t = (a * _sigmoid(a)) * u
    acc_scr[...] += jnp.dot(t.astype(BF16), wd_ref[...], preferred_element_type=F32)

    @pl.when(j == pl.num_programs(1) - 1)
    def _():
        out_ref[...] = x_ref[...] + _rms_norm(acc_scr[...], gpost_ref[...])


def _ffn(x, g_pre, w_gate, w_up, w_down, g_post):
    n = x.shape[0]
    d_ff = w_gate.shape[1]
    tm, tf = ROW_TILE, FFN_TILE
    row_blk = lambda i, j: (i, 0)
    vec = pl.BlockSpec((1, D_MODEL), lambda i, j: (0, 0))
    return pl.pallas_call(
        _ffn_kernel,
        grid=(n // tm, d_ff // tf),
        in_specs=[
            pl.BlockSpec((tm, D_MODEL), row_blk),
            vec,
            pl.BlockSpec((D_MODEL, tf), lambda i, j: (0, j)),
            pl.BlockSpec((D_MODEL, tf), lambda i, j: (0, j)),
            pl.BlockSpec((tf, D_MODEL), lambda i, j: (j, 0)),
            vec,
        ],
        out_specs=pl.BlockSpec((tm, D_MODEL), row_blk),
        out_shape=jax.ShapeDtypeStruct((n, D_MODEL), F32),
        scratch_shapes=[pltpu.VMEM((tm, D_MODEL), BF16), pltpu.VMEM((tm, D_MODEL), F32)],
        compiler_params=_params(("parallel", "arbitrary")),
        name="ffn",
    )(x, g_pre, w_gate, w_up, w_down, g_post)


def _layer(x, pos, past_k, past_v, conv_state, lam_init, wts):
    batch, seq, _ = x.shape
    n = batch * seq
    x2 = x.reshape(n, D_MODEL)
    row = lambda v: v.reshape(1, -1)

    cos_t, sin_lo, sin_hi = _rope_tables(pos)
    if batch > 1:
        cos_t, sin_lo, sin_hi = (jnp.tile(t, (batch, 1)) for t in (cos_t, sin_lo, sin_hi))
    prompt = past_k is None
    q, k_f32, k_bf16, v_f32, v_bf16, glu = _in_proj(
        x2, row(wts["g_pre_mix"]), wts["w_in"], cos_t, sin_lo, sin_hi,
        seq_tiles=seq // ROW_TILE if prompt else None)

    lam_vecs = [row(wts[name]) for name in ("lambda_q1", "lambda_k1", "lambda_q2", "lambda_k2")]
    g_sub = row(wts["g_subln"])
    if prompt:
        a_heads = _attn_prompt(q, k_bf16, v_bf16, lam_vecs, g_sub, batch, seq, lam_init)
        state = jnp.zeros((batch, HALO_ROWS, CONV_CH), F32)
        k_out = k_f32.reshape(batch, N_HEADS, 2, D_QK, seq).transpose(0, 4, 1, 2, 3)
    else:
        past = past_k.shape[1]
        assert (past + seq - 1) // CHUNK <= past // CHUNK and past % SAMPLE_KV_TILE == 0, \
            "sample path assumes every cached and new key is visible to every new query"
        cache_kt = past_k.transpose(0, 2, 3, 4, 1).reshape(batch, N_HEADS, HEAD_COLS, past)
        a_heads = _attn_sample(q, k_bf16, v_bf16, cache_kt,
                               past_v.reshape(batch, past * N_HEADS, D_V), lam_vecs, g_sub, lam_init)
        state = jnp.pad(conv_state, ((0, 0), (HALO_ROWS - CONV_STATE, 0), (0, 0)))
        k_out = k_f32.reshape(batch, seq, N_HEADS, 2, D_QK)

    c_act = _conv_module(glu, state, wts["w_dw"], wts["b_dw"], wts["g_conv_norm"],
                         wts["b_conv_norm"], batch, seq)

    x_mid = _merge(x2, row(wts["g_pre_mix"]), wts["w_in"], a_heads, c_act, wts["w_attn_out"],
                   wts["w_conv_out"], row(wts["b_conv_out"]), wts["w_o"], row(wts["g_post_mix"]))
    y = _ffn(x_mid, row(wts["g_pre_ffn"]), wts["w_ffn_gate"], wts["w_ffn_up"], wts["w_ffn_down"],
             row(wts["g_post_ffn"]))

    glu3 = glu.reshape(batch, seq, CONV_CH)
    if seq >= CONV_STATE:
        new_conv = glu3[:, seq - CONV_STATE:]
    else:
        new_conv = jnp.concatenate([conv_state[:, seq:], glu3], axis=1)
    return (y.reshape(batch, seq, D_MODEL), k_out, v_f32.reshape(batch, seq, N_HEADS, D_V), new_conv)


def kernel(x_prompt, x_sample, cache_k, cache_v, state_conv, g_pre_mix, w_in, lambda_q1, lambda_k1,
           lambda_q2, lambda_k2, g_subln, w_attn_out, w_dw, b_dw, g_conv_norm, b_conv_norm,
           w_conv_out, b_conv_out, w_o, g_post_mix, g_pre_ffn, w_ffn_gate, w_ffn_up, w_ffn_down,
           g_post_ffn):
    depth = w_in.shape[0]
    matmul_weights = dict(w_in=w_in, w_attn_out=w_attn_out, w_conv_out=w_conv_out, w_o=w_o,
                          w_ffn_gate=w_ffn_gate, w_ffn_up=w_ffn_up, w_ffn_down=w_ffn_down)
    other = dict(g_pre_mix=g_pre_mix, lambda_q1=lambda_q1, lambda_k1=lambda_k1, lambda_q2=lambda_q2,
                 lambda_k2=lambda_k2, g_subln=g_subln, w_dw=w_dw, b_dw=b_dw, g_conv_norm=g_conv_norm,
                 b_conv_norm=b_conv_norm, b_conv_out=b_conv_out, g_post_mix=g_post_mix,
                 g_pre_ffn=g_pre_ffn, g_post_ffn=g_post_ffn)
    pos_p = jnp.arange(x_prompt.shape[1])
    pos_s = cache_k.shape[2] + jnp.arange(x_sample.shape[1])
    yp, ys = x_prompt, x_sample
    outs = [[] for _ in range(6)]
    for l in range(depth):
        wts = {name: w[l].astype(BF16) for name, w in matmul_weights.items()}
        wts.update({name: w[l] for name, w in other.items()})
        li = _lambda_init(l)
        yp, kp, vp, cp = _layer(yp, pos_p, None, None, None, li, wts)
        ys, kn, vn, cn = _layer(ys, pos_s, cache_k[l], cache_v[l], state_conv[l], li, wts)
        for dst, val in zip(outs, (kp, vp, cp, kn, vn, cn)):
            dst.append(val)
    k_prompt, v_prompt, conv_prompt, k_sample, v_sample, conv_sample = (jnp.stack(o) for o in outs)
    return (yp, ys, k_prompt, v_prompt, conv_prompt, k_sample, v_sample, conv_sample)
```
